```python
import jax, jax.numpy as jnp
from jax import lax
import numpy as np

D_MODEL = 1024
BATCH = 4
SEQ = 4096
DEPTH = 4
DEC_BATCH = 128
DEC_SEQ = 4
PAST_LEN = 2048
PAGE_SIZE = 128

HGRN_HEADS = 4
HGRN_DK = 128
HGRN_DV = 128
HGRN_CHUNK = 64
HGRN_FDIM = HGRN_HEADS * HGRN_DK
HGRN_VDIM = HGRN_HEADS * HGRN_DV
MOBA_HEADS = 4
MOBA_HEAD_DIM = 128
MOBA_DIM = MOBA_HEADS * MOBA_HEAD_DIM
MOBA_BLOCK = 256
MOBA_TOPK = 3
MOBA_QBLOCK = 32
MOBA_QBLOCK_SAMPLE = 1
ROPE_THETA = 10000.0
D_FF = 4 * D_MODEL
NORM_EPS = 1e-6
MASK_VALUE = -1e30
F_MIN = 1e-30
IN_WIDTHS = (HGRN_FDIM, HGRN_FDIM, HGRN_VDIM, HGRN_VDIM, MOBA_DIM, MOBA_DIM, MOBA_DIM, D_MODEL, D_MODEL)
D_IN = HGRN_FDIM * 2 + HGRN_VDIM * 2 + MOBA_DIM * 3 + D_MODEL * 2

kernel_name = "hgrn2_moba_gated_hybrid_step"


def rmsnorm(x, g):
    xf = x.astype(jnp.float32)
    y = xf * lax.rsqrt(jnp.mean(xf * xf, axis=-1, keepdims=True) + NORM_EPS) * g.astype(jnp.float32)
    return y.astype(x.dtype)


def rope(x, pos):
    dh = x.shape[-1]
    inv = ROPE_THETA ** (-jnp.arange(0, dh, 2, dtype=jnp.float32) / dh)
    ang = pos.astype(jnp.float32)[:, None] * inv[None, :]
    cos = jnp.cos(ang)[None, :, None, :]
    sin = jnp.sin(ang)[None, :, None, :]
    xf = x.astype(jnp.float32)
    x1, x2 = xf[..., : dh // 2], xf[..., dh // 2:]
    return jnp.concatenate([x1 * cos - x2 * sin, x2 * cos + x1 * sin], axis=-1).astype(x.dtype)


def hgrn_scan(q, k, logf, v, s0, chunk):
    B, S, H, DK = q.shape
    nc = S // chunk

    def to_chunks(a):
        return a.reshape(B, nc, chunk, H, a.shape[-1]).swapaxes(0, 1)

    causal = jnp.tril(jnp.ones((chunk, chunk), dtype=bool))[None, :, :, None, None]

    def step(state, inp):
        qc, kc, gc, vc = inp
        b = jnp.cumsum(gc, axis=1)
        o_inter = jnp.einsum('bthk,bhkv->bthv', qc * jnp.exp(b), state)
        diff = b[:, :, None] - b[:, None]
        decay = jnp.where(causal, jnp.exp(jnp.where(causal, diff, 0.0)), 0.0)
        a = jnp.einsum('bthk,bshk,btshk->bths', qc, kc, decay)
        o = o_inter + jnp.einsum('bths,bshv->bthv', a, vc)
        b_last = b[:, -1]
        state = jnp.exp(b_last)[..., None] * state + jnp.einsum(
            'bshk,bshv->bhkv', kc * jnp.exp(b_last[:, None] - b), vc)
        return state, o

    s_final, o = lax.scan(step, s0, (to_chunks(q), to_chunks(k), to_chunks(logf), to_chunks(v)))
    return o.swapaxes(0, 1).reshape(B, S, H, v.shape[-1]), s_final


def hgrn_branch(pq, pf, pi, pg, lb, gnorm, s0):
    B, S = pq.shape[:2]
    q = jax.nn.silu(pq.astype(jnp.float32)).reshape(B, S, HGRN_HEADS, HGRN_DK)
    z = pf.astype(jnp.float32).reshape(B, S, HGRN_HEADS, HGRN_DK)
    lbh = lb.astype(jnp.float32).reshape(HGRN_HEADS, HGRN_DK)
    f = lbh + (1.0 - lbh) * jax.nn.sigmoid(z)
    logf = jnp.log(jnp.maximum(f, F_MIN))
    k = (1.0 - lbh) * jax.nn.sigmoid(-z)
    v = pi.astype(jnp.float32).reshape(B, S, HGRN_HEADS, HGRN_DV)
    chunk = min(HGRN_CHUNK, S)
    o, s_final = hgrn_scan(q, k, logf, v, s0.astype(jnp.float32), chunk)
    gate = jax.nn.silu(pg.astype(jnp.float32)).reshape(B, S, HGRN_HEADS, HGRN_DV)
    o = rmsnorm(o, gnorm) * gate
    return o.reshape(B, S, HGRN_VDIM).astype(pq.dtype), s_final


def moba_attention(q, k, v, q_pos, q_block):
    B, Sq, H, Dh = q.shape
    T = k.shape[1]
    nb = -(-T // MOBA_BLOCK)
    pad = nb * MOBA_BLOCK - T
    kb = jnp.pad(k, ((0, 0), (0, pad), (0, 0), (0, 0))).reshape(B, nb, MOBA_BLOCK, H, Dh)
    vb = jnp.pad(v, ((0, 0), (0, pad), (0, 0), (0, 0))).reshape(B, nb, MOBA_BLOCK, H, Dh)
    kmean = jnp.mean(kb, axis=2, dtype=jnp.float32)
    n_sel = min(MOBA_TOPK, nb)
    scale = Dh ** -0.5
    bi = jnp.arange(B)[:, None, None, None]
    hi = jnp.arange(H)[None, :, None, None]
    block_ids = jnp.arange(nb)
    offs = jnp.arange(MOBA_BLOCK)

    def block_fn(args):
        qc, pc = args
        qb = qc.shape[1]
        own = pc // MOBA_BLOCK
        gs = jnp.einsum('bqhd,bnhd->bhqn', qc.astype(jnp.float32), kmean)
        past = block_ids[None, None, None, :] < own[None, None, :, None]
        gs = jnp.where(past, gs, MASK_VALUE)
        _, top_i = lax.top_k(gs, n_sel)
        valid = top_i < own[None, None, :, None]
        own_b = jnp.broadcast_to(own[None, None, :, None], (B, H, qb, 1))
        blocks = jnp.concatenate([top_i, own_b], axis=-1)
        valid = jnp.concatenate([valid, jnp.ones((B, H, qb, 1), dtype=bool)], axis=-1)
        kg = kb[bi, blocks, :, hi]
        vg = vb[bi, blocks, :, hi]
        s = jnp.einsum('bqhd,bhqnkd->bhqnk', qc, kg, preferred_element_type=jnp.float32) * scale
        kpos = blocks[..., None] * MOBA_BLOCK + offs
        mask = valid[..., None] & (kpos <= pc[None, None, :, None, None])
        s = jnp.where(mask, s, MASK_VALUE).reshape(B, H, qb, -1)
        p = jax.nn.softmax(s, axis=-1).reshape(B, H, qb, n_sel + 1, MOBA_BLOCK)
        return jnp.einsum('bhqnk,bhqnkd->bqhd', p.astype(vg.dtype), vg)

    nq = Sq // q_block
    qs = q.reshape(B, nq, q_block, H, Dh).swapaxes(0, 1)
    ps = q_pos.reshape(nq, q_block)
    out = lax.map(block_fn, (qs, ps))
    return out.swapaxes(0, 1).reshape(B, Sq, H, Dh)


def layer_forward(x, pos, s0, k_past, v_past, q_block,
                  g_mix_pre, g_mix_post, g_ffn_pre, g_ffn_post, w_in, lb, gnorm,
                  w_branch_a, w_branch_b, w_out, w_up, w_down):
    B, S, _ = x.shape
    h = rmsnorm(x, g_mix_pre)
    proj = h @ w_in
    offsets = [int(o) for o in np.cumsum(IN_WIDTHS)[:-1]]
    hq, hf, hi_, hg, mq, mk, mv, ga, gb = jnp.split(proj, offsets, axis=-1)
    ya, s_final = hgrn_branch(hq, hf, hi_, hg, lb, gnorm, s0)
    q = rope(mq.reshape(B, S, MOBA_HEADS, MOBA_HEAD_DIM), pos)
    k = rope(mk.reshape(B, S, MOBA_HEADS, MOBA_HEAD_DIM), pos)
    v = mv.reshape(B, S, MOBA_HEADS, MOBA_HEAD_DIM)
    if k_past is None:
        k_all, v_all = k, v
    else:
        k_all = jnp.concatenate([k_past.astype(k.dtype), k], axis=1)
        v_all = jnp.concatenate([v_past.astype(v.dtype), v], axis=1)
    yb = moba_attention(q, k_all, v_all, pos, q_block).reshape(B, S, MOBA_DIM)
    merged = jax.nn.sigmoid(ga) * (ya @ w_branch_a) + jax.nn.sigmoid(gb) * (yb @ w_branch_b)
    x = x + rmsnorm(merged @ w_out, g_mix_post)
    h = rmsnorm(x, g_ffn_pre)
    u = jnp.square(jax.nn.relu(h @ w_up))
    x = x + rmsnorm(u @ w_down, g_ffn_post)
    return x, k, v, s_final.astype(s0.dtype)


def setup_inputs(seed: int = 0) -> dict:
    key = jax.random.key(seed)
    ks = jax.random.split(key, 20)
    n_pages = PAST_LEN // PAGE_SIZE
    n_pool = (DEC_BATCH * n_pages * 5) // 4
    nrm = jax.random.normal
    f32 = jnp.float32
    x_prompt = nrm(ks[0], (BATCH, SEQ, D_MODEL), f32)
    x_sample = nrm(ks[1], (DEC_BATCH, DEC_SEQ, D_MODEL), f32)
    cache_k = nrm(ks[2], (DEPTH, n_pool, PAGE_SIZE, MOBA_HEADS, MOBA_HEAD_DIM), f32)
    cache_v = nrm(ks[3], (DEPTH, n_pool, PAGE_SIZE, MOBA_HEADS, MOBA_HEAD_DIM), f32)
    state_hgrn = 0.5 * nrm(ks[4], (DEPTH, DEC_BATCH, HGRN_HEADS, HGRN_DK, HGRN_DV), f32)
    page_table = jax.random.permutation(ks[5], n_pool)[: DEC_BATCH * n_pages].reshape(
        DEC_BATCH, n_pages).astype(jnp.int32)
    norm_mix_pre = 1.0 + 0.05 * nrm(ks[6], (DEPTH, D_MODEL), f32)
    norm_mix_post = 1.0 + 0.05 * nrm(ks[7], (DEPTH, D_MODEL), f32)
    norm_ffn_pre = 1.0 + 0.05 * nrm(ks[8], (DEPTH, D_MODEL), f32)
    norm_ffn_post = 1.0 + 0.05 * nrm(ks[9], (DEPTH, D_MODEL), f32)
    w_in = nrm(ks[10], (DEPTH, D_MODEL, D_IN), f32) * D_MODEL ** -0.5
    lower_bounds = 0.1 * nrm(ks[11], (DEPTH, HGRN_FDIM), f32)
    hgrn_gnorm = 1.0 + 0.05 * nrm(ks[12], (DEPTH, HGRN_DV), f32)
    w_branch_a = nrm(ks[13], (DEPTH, HGRN_VDIM, D_MODEL), f32) * HGRN_VDIM ** -0.5
    w_branch_b = nrm(ks[14], (DEPTH, MOBA_DIM, D_MODEL), f32) * MOBA_DIM ** -0.5
    w_out = nrm(ks[15], (DEPTH, D_MODEL, D_MODEL), f32) * D_MODEL ** -0.5
    w_up = nrm(ks[16], (DEPTH, D_MODEL, D_FF), f32) * D_MODEL ** -0.5
    w_down = nrm(ks[17], (DEPTH, D_FF, D_MODEL), f32) * D_FF ** -0.5
    return {"x_prompt": x_prompt, "x_sample": x_sample, "cache_k": cache_k, "cache_v": cache_v,
            "state_hgrn": state_hgrn, "page_table": page_table,
            "norm_mix_pre": norm_mix_pre, "norm_mix_post": norm_mix_post,
            "norm_ffn_pre": norm_ffn_pre, "norm_ffn_post": norm_ffn_post,
            "w_in": w_in, "lower_bounds": lower_bounds, "hgrn_gnorm": hgrn_gnorm,
            "w_branch_a": w_branch_a, "w_branch_b": w_branch_b, "w_out": w_out,
            "w_up": w_up, "w_down": w_down}


def reference(x_prompt, x_sample, cache_k, cache_v, state_hgrn, page_table,
              norm_mix_pre, norm_mix_post, norm_ffn_pre, norm_ffn_post,
              w_in, lower_bounds, hgrn_gnorm, w_branch_a, w_branch_b, w_out, w_up, w_down):
    bp, sp, _ = x_prompt.shape
    db, ds, _ = x_sample.shape
    n_pages = page_table.shape[1]
    past_len = n_pages * cache_k.shape[2]
    pos_p = jnp.arange(sp, dtype=jnp.int32)
    pos_s = past_len + jnp.arange(ds, dtype=jnp.int32)
    p_lb = jax.nn.softmax(lower_bounds.astype(jnp.float32), axis=0)
    lb_all = jnp.cumsum(p_lb, axis=0) - p_lb[0:1]
    s0_prompt = jnp.zeros((bp, HGRN_HEADS, HGRN_DK, HGRN_DV), dtype=state_hgrn.dtype)

    xp, xs = x_prompt, x_sample
    kp_l, vp_l, sp_l, ks_l, vs_l, ss_l = [], [], [], [], [], []
    for l in range(DEPTH):
        wl = (norm_mix_pre[l], norm_mix_post[l], norm_ffn_pre[l], norm_ffn_post[l], w_in[l],
              lb_all[l], hgrn_gnorm[l], w_branch_a[l], w_branch_b[l], w_out[l], w_up[l], w_down[l])
        xp, kp, vp, sfp = layer_forward(xp, pos_p, s0_prompt, None, None, MOBA_QBLOCK, *wl)
        k_past = cache_k[l, page_table].reshape(db, past_len, MOBA_HEADS, MOBA_HEAD_DIM)
        v_past = cache_v[l, page_table].reshape(db, past_len, MOBA_HEADS, MOBA_HEAD_DIM)
        xs, ksn, vsn, sfs = layer_forward(xs, pos_s, state_hgrn[l], k_past, v_past,
                                          MOBA_QBLOCK_SAMPLE, *wl)
        kp_l.append(kp); vp_l.append(vp); sp_l.append(sfp)
        ks_l.append(ksn); vs_l.append(vsn); ss_l.append(sfs)

    k_prompt = jnp.stack(kp_l)
    v_prompt = jnp.stack(vp_l)
    hgrn_prompt = jnp.stack(sp_l)
    k_sample = jnp.stack(ks_l)
    v_sample = jnp.stack(vs_l)
    hgrn_sample = jnp.stack(ss_l)
    return (xp, xs, k_prompt, v_prompt, hgrn_prompt, k_sample, v_sample, hgrn_sample)
```

```python
import functools

import jax
import jax.numpy as jnp
import numpy as np
from jax import lax
from jax.experimental import pallas as pl
from jax.experimental.pallas import tpu as pltpu

F32 = jnp.float32
BF16 = jnp.bfloat16
HIGHEST = lax.Precision.HIGHEST

HEADS = 4
HEAD_DIM = 128
BRANCH = HEADS * HEAD_DIM
HGRN_CHUNK = 64
HGRN_SUB = 16
MOBA_BLOCK = 256
MOBA_TOPK = 3
ROPE_THETA = 10000.0
NORM_EPS = 1e-6
MASK_VALUE = -1e30
F_MIN = 1e-30
NEG_BIG = -1e30

TOKEN_TILE = 512
HGRN_SUPER = 512
HGRN_SAMPLE_BATCH = 16
V7X_VMEM_LIMIT = 56 * 1024 * 1024

_COL = {"hq": 0, "hf": 512, "hi": 1024, "hg": 1536, "mq": 2048, "mk": 2560, "mv": 3072,
        "ga": 3584, "gb": 4608}


def _rms(x, g):
    return x * lax.rsqrt(jnp.mean(x * x, axis=-1, keepdims=True) + NORM_EPS) * g


def _dot(a, b):
    return jnp.dot(a, b, preferred_element_type=F32)


def _dot_nt(a, b):
    return lax.dot_general(a, b, (((1,), (1,)), ((), ())), preferred_element_type=F32)


def _const_spec(shape):
    nd = len(shape)
    return pl.BlockSpec(shape, lambda *_: (0,) * nd, pipeline_mode=pl.Buffered(1))


def _in_proj_body(layer, x_ref, g_ref, w_ref, lbraw_ref, cos_ref, sin_ref,
                  hq_ref, lf_ref, hk_ref, hv_ref, hg_ref, mq_ref, mk_ref, mv_ref, ga_ref, gb_ref):
    xn = _rms(x_ref[...], g_ref[...]).astype(BF16)

    def proj(name, width=BRANCH):
        c0 = _COL[name]
        return _dot(xn, w_ref[:, c0:c0 + width])

    lbraw = lbraw_ref[...]
    e = jnp.exp(lbraw - jnp.max(lbraw, axis=0, keepdims=True))
    p = e / jnp.sum(e, axis=0, keepdims=True)
    if layer == 0:
        lb = jnp.zeros((1, BRANCH), F32)
    else:
        lb = jnp.sum(p[1:layer + 1], axis=0, keepdims=True)

    a = proj("hq")
    hq_ref[...] = (a * jax.nn.sigmoid(a)).astype(BF16)
    z = proj("hf")
    f = lb + (1.0 - lb) * jax.nn.sigmoid(z)
    lf_ref[...] = jnp.log(jnp.maximum(f, F_MIN))
    hk_ref[...] = ((1.0 - lb) * jax.nn.sigmoid(-z)).astype(BF16)
    hv_ref[...] = proj("hi").astype(BF16)
    a = proj("hg")
    hg_ref[...] = (a * jax.nn.sigmoid(a)).astype(BF16)

    cosf = cos_ref[...]
    sinf = sin_ref[...]

    def rope(a, h):
        ah = a[:, h * HEAD_DIM:(h + 1) * HEAD_DIM]
        return ah * cosf + pltpu.roll(ah, HEAD_DIM // 2, axis=1) * sinf

    a = proj("mq")
    for h in range(HEADS):
        mq_ref[:, h * HEAD_DIM:(h + 1) * HEAD_DIM] = rope(a, h).astype(BF16)
    a = proj("mk")
    for h in range(HEADS):
        mk_ref[:, h * HEAD_DIM:(h + 1) * HEAD_DIM] = rope(a, h)
    mv_ref[...] = proj("mv")
    d_model = ga_ref.shape[1]
    ga_ref[...] = jax.nn.sigmoid(proj("ga", d_model)).astype(BF16)
    gb_ref[...] = jax.nn.sigmoid(proj("gb", d_model)).astype(BF16)


def _in_proj(layer, x, g, w, lbraw, cosf, sinf):
    t, d = x.shape
    d_in = w.shape[1]
    tm = TOKEN_TILE
    row = lambda width: pl.BlockSpec((tm, width), lambda i: (i, 0))
    out_shape = [
        jax.ShapeDtypeStruct((t, BRANCH), BF16),
        jax.ShapeDtypeStruct((t, BRANCH), F32),
        jax.ShapeDtypeStruct((t, BRANCH), BF16),
        jax.ShapeDtypeStruct((t, BRANCH), BF16),
        jax.ShapeDtypeStruct((t, BRANCH), BF16),
        jax.ShapeDtypeStruct((t, BRANCH), BF16),
        jax.ShapeDtypeStruct((t, BRANCH), F32),
        jax.ShapeDtypeStruct((t, BRANCH), F32),
        jax.ShapeDtypeStruct((t, d), BF16),
        jax.ShapeDtypeStruct((t, d), BF16),
    ]
    return pl.pallas_call(
        functools.partial(_in_proj_body, layer),
        grid=(t // tm,),
        in_specs=[row(d), _const_spec((1, d)), _const_spec((d, d_in)), _const_spec(lbraw.shape),
                  row(HEAD_DIM), row(HEAD_DIM)],
        out_specs=[row(BRANCH)] * 8 + [row(d)] * 2,
        out_shape=out_shape,
        compiler_params=pltpu.CompilerParams(dimension_semantics=("arbitrary",),
                                             vmem_limit_bytes=V7X_VMEM_LIMIT),
        name="in_proj",
    )(x, g, w, lbraw, cosf, sinf)


def _hgrn_chunk(q, k, v, lf, st, tril, ones_bf):
    c = HGRN_CHUNK
    sub = HGRN_SUB
    b = jnp.dot(tril, lf, precision=HIGHEST, preferred_element_type=F32)
    qe = (q * jnp.exp(b)).astype(BF16)
    o_inter = _dot_nt(qe, st.astype(BF16))
    v_bf = v.astype(BF16)
    rows = lax.broadcasted_iota(jnp.int32, (sub, HEAD_DIM), 0)
    pieces = []
    for i in range(c // sub):
        lo = i * sub
        bi = b[lo:lo + sub]
        qi = q[lo:lo + sub]
        oi = jnp.zeros((sub, HEAD_DIM), F32)
        if i > 0:
            ri = b[lo - 1:lo]
            qh = (qi * jnp.exp(bi - ri)).astype(BF16)
            kh = (k[:lo] * jnp.exp(ri - b[:lo])).astype(BF16)
            a = _dot_nt(qh, kh)
            oi = oi + _dot(a.astype(BF16), v_bf[:lo])
        ms = []
        for s in range(sub):
            bs = b[lo + s:lo + s + 1]
            ks = k[lo + s:lo + s + 1]
            diff = jnp.where(rows >= s, bi - bs, NEG_BIG)
            ms.append((qi * ks * jnp.exp(diff)).astype(BF16))
        r = _dot(jnp.concatenate(ms, axis=0), ones_bf)
        for s in range(sub):
            oi = oi + r[s * sub:(s + 1) * sub] * v[lo + s:lo + s + 1]
        pieces.append(oi)
    o = o_inter + jnp.concatenate(pieces, axis=0)
    b_last = b[c - 1:c]
    kt = (k * jnp.exp(b_last - b)).astype(BF16)
    st_new = jnp.exp(b_last) * st + _dot(v.T.astype(BF16), kt)
    return o, st_new


def _hgrn_prompt_body(q_ref, k_ref, v_ref, lf_ref, g_ref, gn_ref, y_ref, sfin_ref, st_ref):
    c_id = pl.program_id(2)

    @pl.when(c_id == 0)
    def _():
        st_ref[...] = jnp.zeros_like(st_ref)

    c = HGRN_CHUNK
    tril = (lax.broadcasted_iota(jnp.int32, (c, c), 0)
            >= lax.broadcasted_iota(jnp.int32, (c, c), 1)).astype(F32)
    ones_bf = jnp.ones((HEAD_DIM, HEAD_DIM), BF16)
    gn = gn_ref[...]

    def body(ci, st):
        r0 = pl.multiple_of(ci * c, c)
        rows = pl.ds(r0, c)
        o, st_new = _hgrn_chunk(q_ref[rows, :].astype(F32), k_ref[rows, :].astype(F32),
                                v_ref[rows, :].astype(F32), lf_ref[rows, :], st, tril, ones_bf)
        y = _rms(o, gn) * g_ref[rows, :].astype(F32)
        y_ref[rows, :] = y.astype(BF16)
        return st_new

    st = lax.fori_loop(0, q_ref.shape[0] // c, body, st_ref[...])
    st_ref[...] = st

    @pl.when(c_id == pl.num_programs(2) - 1)
    def _():
        sfin_ref[0, 0] = st.T


def _hgrn_prompt(hq, hk, hv, lf, hg, gn, batch, seq):
    sc = HGRN_SUPER
    nsc = seq // sc
    blk = pl.BlockSpec((sc, HEAD_DIM), lambda b, h, c: (b * nsc + c, h))
    return pl.pallas_call(
        _hgrn_prompt_body,
        grid=(batch, HEADS, nsc),
        in_specs=[blk, blk, blk, blk, blk, pl.BlockSpec((1, HEAD_DIM), lambda b, h, c: (0, 0))],
        out_specs=[blk, pl.BlockSpec((1, 1, HEAD_DIM, HEAD_DIM), lambda b, h, c: (b, h, 0, 0))],
        out_shape=[jax.ShapeDtypeStruct((batch * seq, BRANCH), BF16),
                   jax.ShapeDtypeStruct((batch, HEADS, HEAD_DIM, HEAD_DIM), F32)],
        scratch_shapes=[pltpu.VMEM((HEAD_DIM, HEAD_DIM), F32)],
        compiler_params=pltpu.CompilerParams(
            dimension_semantics=("arbitrary", "arbitrary", "arbitrary")),
        name="hgrn_prompt",
    )(hq, hk, hv, lf, hg, gn)


def _hgrn_sample_body(n_tok, q_ref, k_ref, v_ref, lf_ref, g_ref, gn_ref, s_ref, y_ref, so_ref):
    r = q_ref.shape[0]
    nb = r // n_tok
    ri = lax.broadcasted_iota(jnp.int32, (r, r), 0)
    ci = lax.broadcasted_iota(jnp.int32, (r, r), 1)
    same = (ri // n_tok) == (ci // n_tok)
    seg_tril = (same & (ri >= ci)).astype(F32)
    seg_all = same.astype(F32)
    tok = lax.broadcasted_iota(jnp.int32, (r, HEAD_DIM), 0) % n_tok
    col_seq = lax.broadcasted_iota(jnp.int32, (HEAD_DIM, r), 1) // n_tok
    grp_rows = 16
    grp_seq = lax.broadcasted_iota(jnp.int32, (grp_rows, HEAD_DIM), 0) // n_tok
    gn = gn_ref[...]
    for h in range(HEADS):
        cs = slice(h * HEAD_DIM, (h + 1) * HEAD_DIM)
        q = q_ref[:, cs].astype(F32)
        k = k_ref[:, cs].astype(F32)
        v = v_ref[:, cs].astype(F32)
        lf = lf_ref[:, cs]
        b = jnp.dot(seg_tril, lf, precision=HIGHEST, preferred_element_type=F32)
        b_last = jnp.dot(seg_all, lf, precision=HIGHEST, preferred_element_type=F32)
        qe = (q * jnp.exp(b)).astype(BF16)
        o = jnp.zeros((r, HEAD_DIM), F32)
        for dl in range(n_tok):
            if dl == 0:
                kd, bd, vd = k, b, v
                diff = b - bd
            else:
                kd = pltpu.roll(k, dl, axis=0)
                bd = pltpu.roll(b, dl, axis=0)
                vd = pltpu.roll(v, dl, axis=0)
                diff = jnp.where(tok >= dl, b - bd, NEG_BIG)
            w = jnp.sum(q * kd * jnp.exp(diff), axis=-1, keepdims=True)
            o = o + w * vd
        kt_t = (k * jnp.exp(b_last - b)).T
        d_t = jnp.exp(b_last).T
        v_bf = v.astype(BF16)
        inter = []
        for grp in range(r // grp_rows):
            qe_g = qe[grp * grp_rows:(grp + 1) * grp_rows]
            res = jnp.zeros((grp_rows, HEAD_DIM), F32)
            for j in range(grp_rows // n_tok):
                bb = grp * (grp_rows // n_tok) + j
                s0 = s_ref[0, bb, h]
                res = jnp.where(grp_seq == j, _dot(qe_g, s0.astype(BF16)), res)
                kt_m = jnp.where(col_seq == bb, kt_t, 0.0).astype(BF16)
                so_ref[bb, h] = d_t[:, bb * n_tok:bb * n_tok + 1] * s0 + _dot(kt_m, v_bf)
            inter.append(res)
        o = o + jnp.concatenate(inter, axis=0)
        y_ref[:, cs] = (_rms(o, gn) * g_ref[:, cs].astype(F32)).astype(BF16)


def _hgrn_sample(layer, hq, hk, hv, lf, hg, gn, state, row0, n_seq, n_tok):
    nb = HGRN_SAMPLE_BATCH
    r = nb * n_tok
    blk0 = row0 // r
    blk = pl.BlockSpec((r, BRANCH), lambda i: (blk0 + i, 0))
    return pl.pallas_call(
        functools.partial(_hgrn_sample_body, n_tok),
        grid=(n_seq // nb,),
        in_specs=[blk, blk, blk, blk, blk, pl.BlockSpec((1, HEAD_DIM), lambda i: (0, 0)),
                  pl.BlockSpec((1, nb, HEADS, HEAD_DIM, HEAD_DIM), lambda i: (layer, i, 0, 0, 0))],
        out_specs=[pl.BlockSpec((r, BRANCH), lambda i: (i, 0)),
                   pl.BlockSpec((nb, HEADS, HEAD_DIM, HEAD_DIM), lambda i: (i, 0, 0, 0))],
        out_shape=[jax.ShapeDtypeStruct((n_seq * n_tok, BRANCH), BF16),
                   jax.ShapeDtypeStruct((n_seq, HEADS, HEAD_DIM, HEAD_DIM), F32)],
        compiler_params=pltpu.CompilerParams(dimension_semantics=("arbitrary",),
                                             vmem_limit_bytes=V7X_VMEM_LIMIT),
        name="hgrn_sample",
    )(hq, hk, hv, lf, hg, gn, state)


def _topk_select(gm, n_blk, width):
    blk = lax.broadcasted_iota(jnp.int32, (n_blk, width), 0)
    cnt = jnp.zeros((n_blk, width), F32)
    for m in range(n_blk):
        row = gm[m:m + 1, :]
        tie = jnp.where(blk > m, 1.0, 0.0)
        cnt = cnt + jnp.where(row > gm, 1.0, jnp.where(row == gm, tie, 0.0))
    return jnp.where(cnt < MOBA_TOPK, 1.0, 0.0)


def _moba_prompt_body(q_ref, k_ref, v_ref, o_ref, kbf_ref, vbf_ref, kmean_ref, m_ref, l_ref, acc_ref):
    i = pl.program_id(2)
    nb = kmean_ref.shape[0]
    blk = MOBA_BLOCK
    scale = HEAD_DIM ** -0.5

    @pl.when(i == 0)
    def _():
        kf = k_ref[...]
        kbf_ref[...] = kf.astype(BF16)
        vbf_ref[...] = v_ref[...].astype(BF16)
        kmean_ref[...] = jnp.mean(kf.reshape(nb, blk, HEAD_DIM), axis=1)

    q = q_ref[...]
    kmean = kmean_ref[...]
    km_hi = kmean.astype(BF16)
    km_lo = (kmean - km_hi.astype(F32)).astype(BF16)
    gs_t = _dot_nt(km_hi, q) + _dot_nt(km_lo, q)
    blk_id = lax.broadcasted_iota(jnp.int32, (nb, blk), 0)
    past = blk_id < i
    gm = jnp.where(past, gs_t, MASK_VALUE)
    sel_t = jnp.where(past, _topk_select(gm, nb, blk), 0.0)
    eye = (lax.broadcasted_iota(jnp.int32, (blk, blk), 0)
           == lax.broadcasted_iota(jnp.int32, (blk, blk), 1)).astype(BF16)
    sel = _dot_nt(eye, sel_t.astype(BF16)).astype(BF16)

    def attend(j, mask_fn):
        r0 = pl.multiple_of(j * blk, blk)
        kj = kbf_ref[pl.ds(r0, blk), :]
        vj = vbf_ref[pl.ds(r0, blk), :]
        s = jnp.where(mask_fn(), _dot_nt(q, kj) * scale, MASK_VALUE)
        m_old = m_ref[...]
        m_new = jnp.maximum(m_old, jnp.max(s, axis=-1, keepdims=True))
        alpha = jnp.exp(m_old - m_new)
        p = jnp.exp(s - m_new)
        l_ref[...] = alpha * l_ref[...] + jnp.sum(p, axis=-1, keepdims=True)
        acc_ref[...] = alpha * acc_ref[...] + _dot(p.astype(BF16), vj)
        m_ref[...] = m_new

    m_ref[...] = jnp.full(m_ref.shape, MASK_VALUE, F32)
    l_ref[...] = jnp.zeros(l_ref.shape, F32)
    acc_ref[...] = jnp.zeros(acc_ref.shape, F32)
    causal = (lax.broadcasted_iota(jnp.int32, (blk, blk), 0)
              >= lax.broadcasted_iota(jnp.int32, (blk, blk), 1))
    attend(i, lambda: causal)

    def past_body(j, carry):
        onehot = (lax.broadcasted_iota(jnp.int32, (nb, HEAD_DIM), 0) == j).astype(BF16)
        col = _dot(sel, onehot)
        attend(j, lambda: jnp.concatenate([col] * (blk // HEAD_DIM), axis=1) > 0.5)
        return carry

    lax.fori_loop(0, i, past_body, 0)
    o_ref[...] = (acc_ref[...] / l_ref[...]).astype(BF16)


def _moba_prompt(mq, mk, mv, batch, seq):
    nb = seq // MOBA_BLOCK
    return pl.pallas_call(
        _moba_prompt_body,
        grid=(batch, HEADS, nb),
        in_specs=[pl.BlockSpec((MOBA_BLOCK, HEAD_DIM), lambda b, h, i: (b * nb + i, h)),
                  pl.BlockSpec((seq, HEAD_DIM), lambda b, h, i: (b, h)),
                  pl.BlockSpec((seq, HEAD_DIM), lambda b, h, i: (b, h))],
        out_specs=pl.BlockSpec((MOBA_BLOCK, HEAD_DIM), lambda b, h, i: (b * nb + i, h)),
        out_shape=jax.ShapeDtypeStruct((batch * seq, BRANCH), BF16),
        scratch_shapes=[pltpu.VMEM((seq, HEAD_DIM), BF16), pltpu.VMEM((seq, HEAD_DIM), BF16),
                        pltpu.VMEM((nb, HEAD_DIM), F32), pltpu.VMEM((MOBA_BLOCK, 1), F32),
                        pltpu.VMEM((MOBA_BLOCK, 1), F32), pltpu.VMEM((MOBA_BLOCK, HEAD_DIM), F32)],
        compiler_params=pltpu.CompilerParams(
            dimension_semantics=("arbitrary", "arbitrary", "arbitrary")),
        name="moba_prompt",
    )(mq, mk, mv)


def _moba_sample_body(layer, n_pages, n_tok, pt_ref, q_ref, kn_ref, vn_ref, ck_ref, cv_ref, o_ref,
                      kbuf, vbuf, sems):
    b = pl.program_id(0)
    nbatch = pl.num_programs(0)
    page_rows = ck_ref.shape[2]
    n_rows = n_pages * page_rows
    blk_rows = MOBA_BLOCK * HEADS
    n_past = n_rows // blk_rows
    nq = HEADS * n_tok
    scale = HEAD_DIM ** -0.5

    def page_copy(seq, slot, p, which):
        src = (ck_ref, cv_ref)[which]
        dst = (kbuf, vbuf)[which]
        return pltpu.make_async_copy(src.at[layer, pt_ref[seq, p]],
                                     dst.at[slot, pl.ds(p * page_rows, page_rows)],
                                     sems.at[slot, which])

    def start_all(seq, slot):
        for p in range(n_pages):
            page_copy(seq, slot, p, 0).start()
            page_copy(seq, slot, p, 1).start()

    @pl.when(b == 0)
    def _():
        start_all(0, 0)

    slot = b % 2

    @pl.when(b + 1 < nbatch)
    def _():
        start_all(b + 1, 1 - slot)

    for p in range(n_pages):
        page_copy(b, slot, p, 0).wait()
        page_copy(b, slot, p, 1).wait()

    q = q_ref[0]
    kc = kbuf[slot].astype(BF16)
    vc = vbuf[slot].astype(BF16)
    s_raw = _dot_nt(q, kc)
    own_head = (lax.broadcasted_iota(jnp.int32, (nq, blk_rows), 0) // n_tok
                == lax.broadcasted_iota(jnp.int32, (nq, blk_rows), 1) % HEADS)
    s_blk = [jnp.where(own_head, s_raw[:, n * blk_rows:(n + 1) * blk_rows], 0.0) for n in range(n_past)]
    gates = [jnp.sum(s, axis=-1, keepdims=True) * (1.0 / MOBA_BLOCK) for s in s_blk]
    pieces = []
    for n in range(n_past):
        cnt = jnp.zeros((nq, 1), F32)
        for m in range(n_past):
            if m == n:
                continue
            beats = gates[m] >= gates[n] if m < n else gates[m] > gates[n]
            cnt = cnt + jnp.where(beats, 1.0, 0.0)
        picked = jnp.where(cnt < MOBA_TOPK, s_blk[n] * scale, MASK_VALUE)
        pieces.append(jnp.where(own_head, picked, MASK_VALUE))
    s_past = jnp.concatenate(pieces, axis=1)
    kn = kn_ref[0]
    s_own = _dot_nt(q, kn.astype(BF16)) * scale
    r_i = lax.broadcasted_iota(jnp.int32, (nq, nq), 0)
    c_i = lax.broadcasted_iota(jnp.int32, (nq, nq), 1)
    own_ok = ((r_i // n_tok) == (c_i // n_tok)) & ((c_i % n_tok) <= (r_i % n_tok))
    s_own = jnp.where(own_ok, s_own, MASK_VALUE)
    m = jnp.maximum(jnp.max(s_past, axis=-1, keepdims=True), jnp.max(s_own, axis=-1, keepdims=True))
    p_past = jnp.exp(s_past - m)
    p_own = jnp.exp(s_own - m)
    l = jnp.sum(p_past, axis=-1, keepdims=True) + jnp.sum(p_own, axis=-1, keepdims=True)
    acc = _dot(p_past.astype(BF16), vc) + _dot(p_own.astype(BF16), vn_ref[0].astype(BF16))
    o_ref[0] = acc / l


def _moba_sample(layer, page_table, q16, kn16, vn16, cache_k, cache_v):
    n_seq, nq, _ = q16.shape
    n_tok = nq // HEADS
    n_pages = page_table.shape[1]
    page_rows = cache_k.shape[2]
    row_blk = lambda: pl.BlockSpec((1, nq, HEAD_DIM), lambda b, pt: (b, 0, 0))
    grid_spec = pltpu.PrefetchScalarGridSpec(
        num_scalar_prefetch=1,
        grid=(n_seq,),
        in_specs=[row_blk(), row_blk(), row_blk(),
                  pl.BlockSpec(memory_space=pl.ANY), pl.BlockSpec(memory_space=pl.ANY)],
        out_specs=row_blk(),
        scratch_shapes=[pltpu.VMEM((2, n_pages * page_rows, HEAD_DIM), F32),
                        pltpu.VMEM((2, n_pages * page_rows, HEAD_DIM), F32),
                        pltpu.SemaphoreType.DMA((2, 2))],
    )
    return pl.pallas_call(
        functools.partial(_moba_sample_body, layer, n_pages, n_tok),
        grid_spec=grid_spec,
        out_shape=jax.ShapeDtypeStruct((n_seq, nq, HEAD_DIM), F32),
        compiler_params=pltpu.CompilerParams(dimension_semantics=("arbitrary",),
                                             vmem_limit_bytes=V7X_VMEM_LIMIT),
        name="moba_sample",
    )(page_table, q16, kn16, vn16, cache_k, cache_v)


def _mix_ffn_body(x_ref, ya_ref, yb_ref, ga_ref, gb_ref, wa_ref, wb_ref, wo_ref, wup_ref, wdn_ref,
                  gpost_ref, gfpre_ref, gfpost_ref, o_ref):
    merged = (ga_ref[...].astype(F32) * _dot(ya_ref[...], wa_ref[...])
              + gb_ref[...].astype(F32) * _dot(yb_ref[...], wb_ref[...]))
    x1 = x_ref[...] + _rms(_dot(merged.astype(BF16), wo_ref[...]), gpost_ref[...])
    h = _rms(x1, gfpre_ref[...]).astype(BF16)
    d_ff = wup_ref.shape[1]
    step = 1024
    acc = jnp.zeros(x1.shape, F32)
    for c0 in range(0, d_ff, step):
        u = jnp.maximum(_dot(h, wup_ref[:, c0:c0 + step]), 0.0)
        acc = acc + _dot((u * u).astype(BF16), wdn_ref[c0:c0 + step, :])
    o_ref[...] = x1 + _rms(acc, gfpost_ref[...])


def _mix_ffn(x, ya, yb, ga, gb, wa, wb, wo, wup, wdn, gpost, gfpre, gfpost):
    t, d = x.shape
    tm = TOKEN_TILE
    row = lambda width: pl.BlockSpec((tm, width), lambda i: (i, 0))
    return pl.pallas_call(
        _mix_ffn_body,
        grid=(t // tm,),
        in_specs=[row(d), row(BRANCH), row(BRANCH), row(d), row(d),
                  _const_spec(wa.shape), _const_spec(wb.shape), _const_spec(wo.shape),
                  _const_spec(wup.shape), _const_spec(wdn.shape),
                  _const_spec((1, d)), _const_spec((1, d)), _const_spec((1, d))],
        out_specs=row(d),
        out_shape=jax.ShapeDtypeStruct((t, d), F32),
        compiler_params=pltpu.CompilerParams(dimension_semantics=("arbitrary",),
                                             vmem_limit_bytes=V7X_VMEM_LIMIT),
        name="mix_ffn",
    )(x, ya, yb, ga, gb, wa, wb, wo, wup, wdn, gpost, gfpre, gfpost)


def _rope_tables(pos):
    inv = ROPE_THETA ** (-jnp.arange(0, HEAD_DIM, 2, dtype=F32) / HEAD_DIM)
    ang = pos.astype(F32)[:, None] * inv[None, :]
    cos, sin = jnp.cos(ang), jnp.sin(ang)
    return jnp.concatenate([cos, cos], axis=-1), jnp.concatenate([-sin, sin], axis=-1)


def kernel(x_prompt, x_sample, cache_k, cache_v, state_hgrn, page_table, norm_mix_pre, norm_mix_post,
           norm_ffn_pre, norm_ffn_post, w_in, lower_bounds, hgrn_gnorm, w_branch_a, w_branch_b, w_out,
           w_up, w_down):
    bp, sp, d = x_prompt.shape
    db, ds, _ = x_sample.shape
    depth, n_pool, page_size = cache_k.shape[:3]
    n_pages = page_table.shape[1]
    past_len = n_pages * page_size
    tp, ts = bp * sp, db * ds
    assert cache_k.shape[3:] == (HEADS, HEAD_DIM) and w_in.shape[2] == 7 * BRANCH + 2 * d
    assert tp % TOKEN_TILE == 0 and ts % TOKEN_TILE == 0 and sp % HGRN_SUPER == 0
    assert sp % MOBA_BLOCK == 0 and past_len % MOBA_BLOCK == 0 and ds <= HGRN_CHUNK
    assert db % HGRN_SAMPLE_BATCH == 0 and HGRN_SAMPLE_BATCH % 2 == 0 and tp % (HGRN_SAMPLE_BATCH * ds) == 0
    assert ds < MOBA_BLOCK

    x = jnp.concatenate([x_prompt.reshape(tp, d), x_sample.reshape(ts, d)], axis=0)
    pos = jnp.concatenate([jnp.tile(jnp.arange(sp, dtype=jnp.int32), bp),
                           jnp.tile(past_len + jnp.arange(ds, dtype=jnp.int32), db)])
    cosf, sinf = _rope_tables(pos)
    bf = lambda w: w.astype(BF16)
    w_in_b, wa_b, wb_b, wo_b, wup_b, wdn_b = map(bf, (w_in, w_branch_a, w_branch_b, w_out, w_up, w_down))
    ck = cache_k.reshape(depth, n_pool, page_size * HEADS, HEAD_DIM)
    cv = cache_v.reshape(depth, n_pool, page_size * HEADS, HEAD_DIM)
    lbraw = lower_bounds.astype(F32)

    def to_heads(a):
        return a.reshape(db, ds, HEADS, HEAD_DIM).transpose(0, 2, 1, 3).reshape(db, HEADS * ds, HEAD_DIM)

    kp_l, vp_l, sp_l, ks_l, vs_l, ss_l = [], [], [], [], [], []
    for l in range(depth):
        hq, lf, hk, hv, hg, mq, mk, mv, ga, gb = _in_proj(
            l, x, norm_mix_pre[l][None], w_in_b[l], lbraw, cosf, sinf)
        gn = hgrn_gnorm[l][None].astype(F32)
        ya_p, sfin_p = _hgrn_prompt(hq, hk, hv, lf, hg, gn, bp, sp)
        ya_s, sfin_s = _hgrn_sample(l, hq, hk, hv, lf, hg, gn, state_hgrn, tp, db, ds)
        yb_p = _moba_prompt(mq, mk, mv, bp, sp)
        o16 = _moba_sample(l, page_table, to_heads(mq[tp:]), to_heads(mk[tp:]), to_heads(mv[tp:]), ck, cv)
        yb_s = o16.reshape(db, HEADS, ds, HEAD_DIM).transpose(0, 2, 1, 3).reshape(ts, BRANCH).astype(BF16)
        ya = jnp.concatenate([ya_p, ya_s], axis=0)
        yb = jnp.concatenate([yb_p, yb_s], axis=0)
        x = _mix_ffn(x, ya, yb, ga, gb, wa_b[l], wb_b[l], wo_b[l], wup_b[l], wdn_b[l],
                     norm_mix_post[l][None], norm_ffn_pre[l][None], norm_ffn_post[l][None])
        kp_l.append(mk[:tp].reshape(bp, sp, HEADS, HEAD_DIM))
        vp_l.append(mv[:tp].reshape(bp, sp, HEADS, HEAD_DIM))
        ks_l.append(mk[tp:].reshape(db, ds, HEADS, HEAD_DIM))
        vs_l.append(mv[tp:].reshape(db, ds, HEADS, HEAD_DIM))
        sp_l.append(sfin_p)
        ss_l.append(sfin_s.astype(state_hgrn.dtype))

    return (x[:tp].reshape(bp, sp, d), x[tp:].reshape(db, ds, d), jnp.stack(kp_l), jnp.stack(vp_l),
            jnp.stack(sp_l), jnp.stack(ks_l), jnp.stack(vs_l), jnp.stack(ss_l))
```
